```python
import jax, jax.numpy as jnp
from jax import lax
import numpy as np

D_MODEL = 2048
BATCH = 4
SEQ = 4096
DEPTH = 1

GRID_W = 64
CTX_LEN = 256
D_CONV = D_MODEL // 2
D_RWKV = D_MODEL // 2
D_MIX = D_CONV + D_RWKV
HEAD_SIZE = 64
N_HEADS_RWKV = D_RWKV // HEAD_SIZE
CONV_WIDTH = 3
W_LORA = 64
A_LORA = 64
G_LORA = 160
P_STATE = 2 * D_RWKV + 2 * W_LORA + 2 * A_LORA
P_RWKV = P_STATE + D_RWKV + G_LORA
P_IN = 3 * D_CONV + P_RWKV
N_EXPERTS = 32
TOP_K = 4
D_EXPERT = D_MODEL
SWIGLU_LIMIT = 7.0
SWIGLU_ALPHA = 1.702
EXPERT_BLOCK = 256
RMS_EPS = 1e-6
LNX_EPS = 64e-5
KK_EPS = 1e-24

kernel_name = 'hybrid_conv_rwkv7_moe_diffusion_block'


def rms_norm(x, g):
    xf = x.astype(jnp.float32)
    y = xf * lax.rsqrt(jnp.mean(xf * xf, axis=-1, keepdims=True) + RMS_EPS)
    return (y * g).astype(x.dtype)


def modulate(x, shift, scale):
    return x * (1.0 + scale) + shift


def split_heads(z):
    return z.reshape(z.shape[:-1] + (N_HEADS_RWKV, HEAD_SIZE))


def token_shift(p, mu):
    prev = jnp.pad(p[:, :-1], ((0, 0), (1, 0), (0, 0)))
    nxt = jnp.pad(p[:, 1:], ((0, 0), (0, 1), (0, 0)))
    return p + mu[0] * (prev - p) + mu[1] * (nxt - p)


def conv3(u_pad, w):
    return u_pad[..., :-2, :] * w[0] + u_pad[..., 1:-1, :] * w[1] + u_pad[..., 2:, :] * w[2]


def short_conv_latent(u, w):
    bsz, n_tok, ch = u.shape
    rows = n_tok // GRID_W
    grid = jnp.pad(u.reshape(bsz, rows, GRID_W, ch), ((0, 0), (0, 0), (1, 1), (0, 0)))
    return conv3(grid, w).reshape(bsz, n_tok, ch)


def short_conv_context(u, w):
    return conv3(jnp.pad(u, ((0, 0), (1, 1), (0, 0))), w)


def conv_group(p, w, conv_fn):
    gate_b = p[..., :D_CONV]
    gate_c = p[..., D_CONV:2 * D_CONV]
    u = p[..., 2 * D_CONV:3 * D_CONV]
    return gate_b * conv_fn(gate_c * u, w)


def rwkv_state_inputs(q, w0, w_up, a0, a_up, k_k, k_a):
    bsz, n_tok, _ = q.shape
    k = q[..., :D_RWKV]
    v = q[..., D_RWKV:2 * D_RWKV]
    off = 2 * D_RWKV
    w_low = q[..., off:off + 2 * W_LORA].reshape(bsz, n_tok, 2, W_LORA)
    off += 2 * W_LORA
    a_low = q[..., off:off + 2 * A_LORA].reshape(bsz, n_tok, 2, A_LORA)
    w_log = w0 + jnp.einsum('btdl,dlc->btdc', jnp.tanh(w_low), w_up)
    w_log = -jax.nn.softplus(-w_log.astype(jnp.float32)) - 0.5
    decay = jnp.exp(-jnp.exp(w_log))
    a = jax.nn.sigmoid(a0 + jnp.einsum('btdl,dlc->btdc', a_low, a_up))
    kk = split_heads(k * k_k).astype(jnp.float32)
    kk = kk * lax.rsqrt(jnp.maximum(jnp.sum(kk * kk, axis=-1, keepdims=True), KK_EPS))
    k_dir = k[:, :, None, :] * (1.0 + (a - 1.0) * k_a)
    return (split_heads(decay), split_heads(k_dir), split_heads(v), kk, split_heads(a))


def wkv_scan(decay, k, a_vec, b_vec, v, r, s0, reverse):
    seq = [jnp.moveaxis(z, 1, 0) for z in (decay, k, a_vec, b_vec, v)]
    emit = r is not None
    if emit:
        seq.append(jnp.moveaxis(r, 1, 0))

    def step(s, inp):
        w_t, k_t, a_t, b_t, v_t = inp[:5]
        sa = jnp.einsum('bhvk,bhk->bhv', s, a_t)
        s = s * w_t[:, :, None, :] + sa[..., None] * b_t[:, :, None, :] + v_t[..., None] * k_t[:, :, None, :]
        y = jnp.einsum('bhvk,bhk->bhv', s, inp[5]) if emit else None
        return s, y

    s, ys = lax.scan(step, s0, tuple(seq), reverse=reverse)
    return s, (jnp.moveaxis(ys, 0, 1) if emit else None)


def bidirectional_wkv(inputs, r, s_init):
    decay, k_dir, v, kk, a = inputs
    states, outs = [], []
    for d, reverse in enumerate((False, True)):
        s, y = wkv_scan(decay[:, :, d], k_dir[:, :, d], -kk, kk * a[:, :, d], v, r, s_init[d], reverse)
        states.append(s)
        outs.append(y)
    return states, outs


def rwkv_readout(outs, r, inputs, r_k, lnx_w, lnx_b, g):
    _, k_dir, v, _, _ = inputs
    o = None
    for d in range(2):
        y = outs[d]
        mu = jnp.mean(y, axis=-1, keepdims=True)
        var = jnp.mean(jnp.square(y - mu), axis=-1, keepdims=True)
        o_d = (y - mu) * lax.rsqrt(var + LNX_EPS) * split_heads(lnx_w) + split_heads(lnx_b)
        o_d = o_d + jnp.sum(r * k_dir[:, :, d] * r_k, axis=-1, keepdims=True) * v
        o = o_d if o is None else o + o_d
    return o.reshape(o.shape[:2] + (D_RWKV,)).astype(g.dtype) * g


def mixer_output(p, q, r, inputs, outs, conv_fn, conv_w, r_k, lnx_w, lnx_b, g_up, w_out):
    y_conv = conv_group(p, conv_w, conv_fn)
    g = jax.nn.sigmoid(q[..., P_STATE + D_RWKV:]) @ g_up
    y_rwkv = rwkv_readout(outs, r, inputs, r_k, lnx_w, lnx_b, g)
    return jnp.concatenate([y_conv, y_rwkv], axis=-1) @ w_out


def clamped_swiglu(gu):
    x_glu = jnp.minimum(gu[..., ::2], SWIGLU_LIMIT)
    x_lin = jnp.clip(gu[..., 1::2], -SWIGLU_LIMIT, SWIGLU_LIMIT)
    return x_glu * jax.nn.sigmoid(SWIGLU_ALPHA * x_glu) * (x_lin + 1.0)


def moe_ffn(h, w_router, b_router, w_gu, b_gu, w_dn, b_dn):
    bsz, n_tok_seq, d = h.shape
    tokens = h.reshape(-1, d)
    n_tok = tokens.shape[0]
    logits = (tokens @ w_router + b_router).astype(jnp.float32)
    top_val, top_idx = lax.top_k(logits, TOP_K)
    gate = jax.nn.softmax(top_val, axis=-1)
    n_assign = n_tok * TOP_K
    e_flat = top_idx.reshape(-1)
    tok_flat = jnp.arange(n_assign, dtype=jnp.int32) // TOP_K
    order = jnp.argsort(e_flat)
    e_sorted = e_flat[order]
    counts = jnp.bincount(e_flat, length=N_EXPERTS)
    padded = ((counts + EXPERT_BLOCK - 1) // EXPERT_BLOCK) * EXPERT_BLOCK
    start = jnp.cumsum(counts) - counts
    pstart = jnp.cumsum(padded) - padded
    dest = pstart[e_sorted] + (jnp.arange(n_assign, dtype=jnp.int32) - start[e_sorted])
    n_blocks = -(-n_assign // EXPERT_BLOCK) + N_EXPERTS
    slots = n_blocks * EXPERT_BLOCK
    slot_tok = jnp.zeros((slots,), jnp.int32).at[dest].set(tok_flat[order])
    slot_w = jnp.zeros((slots,), jnp.float32).at[dest].set(gate.reshape(-1)[order])
    pend = jnp.cumsum(padded)
    block_first = jnp.arange(n_blocks, dtype=jnp.int32) * EXPERT_BLOCK
    block_e = jnp.minimum(jnp.sum(block_first[:, None] >= pend[None, :], axis=-1), N_EXPERTS - 1)

    def block(acc, inp):
        tok, wgt, e = inp
        xb = tokens[tok]
        act = clamped_swiglu(xb @ w_gu[e] + b_gu[e])
        y = act @ w_dn[e] + b_dn[e]
        return acc.at[tok].add(y * wgt[:, None]), None

    acc, _ = lax.scan(block, jnp.zeros((n_tok, d), jnp.float32),
                      (slot_tok.reshape(n_blocks, EXPERT_BLOCK), slot_w.reshape(n_blocks, EXPERT_BLOCK), block_e))
    return acc.reshape(bsz, n_tok_seq, d).astype(h.dtype)


def setup_inputs(seed: int = 0) -> dict:
    key = jax.random.key(seed)
    ks = jax.random.split(key, 27)
    f32 = jnp.float32
    L = DEPTH

    def nrm(k, shape, s):
        return jax.random.normal(k, shape, f32) * s

    return {
        'x': nrm(ks[0], (BATCH, SEQ, D_MODEL), 1.0),
        'c': nrm(ks[1], (BATCH, D_MODEL), 1.0),
        'ctx': nrm(ks[2], (BATCH, CTX_LEN, D_MODEL), 1.0),
        'c_ctx': nrm(ks[3], (D_MODEL,), 1.0),
        'w_mod': nrm(ks[4], (L, D_MODEL, 6 * D_MODEL), D_MODEL ** -0.5),
        'b_mod': nrm(ks[5], (L, 6 * D_MODEL), 0.02),
        'norm_g': 1.0 + nrm(ks[6], (L, 4, D_MODEL), 0.1),
        'w_in': nrm(ks[7], (L, D_MODEL, P_IN), D_MODEL ** -0.5),
        'conv_w': nrm(ks[8], (L, CONV_WIDTH, D_CONV), CONV_WIDTH ** -0.5),
        'mu_shift': jax.random.uniform(ks[9], (L, 2, P_RWKV), f32, 0.0, 0.5),
        'w0': jax.random.uniform(ks[10], (L, 2, D_RWKV), f32, -6.5, -1.5),
        'w_up': nrm(ks[11], (L, 2, W_LORA, D_RWKV), 0.5 * W_LORA ** -0.5),
        'a0': nrm(ks[12], (L, 2, D_RWKV), 0.5),
        'a_up': nrm(ks[13], (L, 2, A_LORA, D_RWKV), 0.5 * A_LORA ** -0.5),
        'k_k': 0.85 + nrm(ks[14], (L, D_RWKV), 0.1),
        'k_a': 1.0 + nrm(ks[15], (L, D_RWKV), 0.1),
        'r_k': nrm(ks[16], (L, N_HEADS_RWKV, HEAD_SIZE), 0.1),
        'lnx_w': 1.0 + nrm(ks[17], (L, D_RWKV), 0.1),
        'lnx_b': nrm(ks[18], (L, D_RWKV), 0.02),
        'g_up': nrm(ks[19], (L, G_LORA, D_RWKV), G_LORA ** -0.5),
        'w_out': nrm(ks[20], (L, D_MIX, D_MODEL), D_MIX ** -0.5),
        'w_router': nrm(ks[21], (L, D_MODEL, N_EXPERTS), D_MODEL ** -0.5),
        'b_router': nrm(ks[22], (L, N_EXPERTS), 0.01),
        'w_gu': nrm(ks[23], (L, N_EXPERTS, D_MODEL, 2 * D_EXPERT), D_MODEL ** -0.5),
        'b_gu': nrm(ks[24], (L, N_EXPERTS, 2 * D_EXPERT), 0.02),
        'w_dn': nrm(ks[25], (L, N_EXPERTS, D_EXPERT, D_MODEL), D_EXPERT ** -0.5),
        'b_dn': nrm(ks[26], (L, N_EXPERTS, D_MODEL), 0.02),
    }


def reference(x, c, ctx, c_ctx, w_mod, b_mod, norm_g, w_in, conv_w, mu_shift, w0, w_up, a0, a_up,
              k_k, k_a, r_k, lnx_w, lnx_b, g_up, w_out, w_router, b_router, w_gu, b_gu, w_dn, b_dn):
    bsz = x.shape[0]
    for layer in range(DEPTH):
        last = layer == DEPTH - 1
        g_pre_mix, g_post_mix, g_pre_ffn, g_post_ffn = norm_g[layer]
        mod_lat = (jax.nn.silu(c) @ w_mod[layer] + b_mod[layer])[:, None, :]
        mod_ctx = (jax.nn.silu(c_ctx) @ w_mod[layer] + b_mod[layer])[None, None, :]
        shift1, scale1, gate1, shift2, scale2, gate2 = jnp.split(mod_lat, 6, axis=-1)
        cshift1, cscale1, cgate1, cshift2, cscale2, cgate2 = jnp.split(mod_ctx, 6, axis=-1)
        rw = (w0[layer], w_up[layer], a0[layer], a_up[layer], k_k[layer], k_a[layer])
        head_p = (r_k[layer], lnx_w[layer], lnx_b[layer], g_up[layer], w_out[layer])
        experts = (w_router[layer], b_router[layer], w_gu[layer], b_gu[layer], w_dn[layer], b_dn[layer])

        h_ctx = modulate(rms_norm(ctx, g_pre_mix), cshift1, cscale1)
        zero_state = jnp.zeros((bsz, N_HEADS_RWKV, HEAD_SIZE, HEAD_SIZE), jnp.float32)
        if last:
            q_ctx = token_shift(h_ctx @ w_in[layer][:, 3 * D_CONV:3 * D_CONV + P_STATE],
                                mu_shift[layer][:, :P_STATE])
            r_ctx = None
        else:
            p_ctx = h_ctx @ w_in[layer]
            q_ctx = token_shift(p_ctx[..., 3 * D_CONV:], mu_shift[layer])
            r_ctx = split_heads(q_ctx[..., P_STATE:P_STATE + D_RWKV])
        in_ctx = rwkv_state_inputs(q_ctx[..., :P_STATE], *rw)
        ctx_states, outs_ctx = bidirectional_wkv(in_ctx, r_ctx, (zero_state, zero_state))

        h = modulate(rms_norm(x, g_pre_mix), shift1, scale1)
        p = h @ w_in[layer]
        q = token_shift(p[..., 3 * D_CONV:], mu_shift[layer])
        in_lat = rwkv_state_inputs(q[..., :P_STATE], *rw)
        r = split_heads(q[..., P_STATE:P_STATE + D_RWKV])
        _, outs = bidirectional_wkv(in_lat, r, ctx_states)
        mix = mixer_output(p, q, r, in_lat, outs, short_conv_latent, conv_w[layer], *head_p)
        x = x + gate1 * rms_norm(mix, g_post_mix)
        ffn = moe_ffn(modulate(rms_norm(x, g_pre_ffn), shift2, scale2), *experts)
        x = x + gate2 * rms_norm(ffn, g_post_ffn)

        if not last:
            mix_c = mixer_output(p_ctx, q_ctx, r_ctx, in_ctx, outs_ctx, short_conv_context, conv_w[layer], *head_p)
            ctx = ctx + cgate1 * rms_norm(mix_c, g_post_mix)
            ffn_c = moe_ffn(modulate(rms_norm(ctx, g_pre_ffn), cshift2, cscale2), *experts)
            ctx = ctx + cgate2 * rms_norm(ffn_c, g_post_ffn)
    return x
```

```python
import functools

import jax
import jax.numpy as jnp
from jax import lax
from jax.experimental import pallas as pl
from jax.experimental.pallas import tpu as pltpu

F32 = jnp.float32
BF16 = jnp.bfloat16

HEAD = 64
GRID_W = 64
CONV_TAPS = 3
W_LORA = 64
A_LORA = 64
G_LORA = 160
N_EXPERTS = 32
TOP_K = 4
SWIGLU_LIMIT = 7.0
SWIGLU_ALPHA = 1.702
RMS_EPS = 1e-6
LNX_EPS = 64e-5
KK_EPS = 1e-24

LANE = 128
CHUNK = 64
GROUP = 256
TOK_TILE = 256
N_TILE = 512
MOE_TM = 512
MOE_TN = 512
GATHER_ROWS = 256
COMBINE_TOK = 64
VMEM_LIMIT = 56 * 1024 * 1024


def _cparams(sem, vmem=VMEM_LIMIT):
    return pltpu.CompilerParams(dimension_semantics=sem, vmem_limit_bytes=vmem)


def _dot(a, b):
    return jnp.dot(a, b, preferred_element_type=F32)


def _dot_nt(a, b):
    return lax.dot_general(a, b, (((1,), (1,)), ((), ())), preferred_element_type=F32)


def _dot_tn(a, b):
    return lax.dot_general(a, b, (((0,), (0,)), ((), ())), preferred_element_type=F32)


def _split_dot(x, m):
    hi = x.astype(BF16)
    lo = (x - hi.astype(F32)).astype(BF16)
    return _dot(hi, m) + _dot(lo, m)


def _rms(x, g):
    return x * lax.rsqrt(jnp.mean(x * x, axis=-1, keepdims=True) + RMS_EPS) * g


def _sigmoid(z):
    return 1.0 / (1.0 + jnp.exp(-z))


def _mod_kernel(c_ref, w_ref, b_ref, o_ref):
    c = c_ref[...]
    act = (c * _sigmoid(c)).astype(BF16)
    o_ref[...] = _dot(act, w_ref[...].astype(BF16)) + b_ref[...]


def _modulation(c_rows, w_mod, b_mod):
    rows, d = c_rows.shape
    n = w_mod.shape[1]
    tn = 1024
    return pl.pallas_call(
        _mod_kernel,
        grid=(n // tn,),
        in_specs=[pl.BlockSpec((rows, d), lambda j: (0, 0)),
                  pl.BlockSpec((d, tn), lambda j: (0, j)),
                  pl.BlockSpec((1, tn), lambda j: (0, j))],
        out_specs=pl.BlockSpec((rows, tn), lambda j: (0, j)),
        out_shape=jax.ShapeDtypeStruct((rows, n), F32),
        compiler_params=_cparams(("parallel",)),
        name="modulation",
    )(c_rows, w_mod, b_mod.reshape(1, n))


def _inproj_kernel(x_ref, g_ref, ss_ref, w_ref, o_ref, h_ref):
    @pl.when(pl.program_id(2) == 0)
    def _():
        ss = ss_ref[0, 0]
        h = _rms(x_ref[0], g_ref[...]) * (1.0 + ss[1:2]) + ss[0:1]
        h_ref[...] = h.astype(BF16)

    o_ref[0] = _dot(h_ref[...], w_ref[...])


def _in_proj(xc, g_pre, ss_tab, w_pad):
    bsz, s, d = xc.shape
    p = w_pad.shape[1]
    return pl.pallas_call(
        _inproj_kernel,
        grid=(bsz, s // TOK_TILE, p // N_TILE),
        in_specs=[pl.BlockSpec((1, TOK_TILE, d), lambda b, i, j: (b, i, 0)),
                  pl.BlockSpec((1, d), lambda b, i, j: (0, 0)),
                  pl.BlockSpec((1, 1, 2, d), lambda b, i, j: (b, jnp.minimum(i, 1), 0, 0)),
                  pl.BlockSpec((d, N_TILE), lambda b, i, j: (0, j))],
        out_specs=pl.BlockSpec((1, TOK_TILE, N_TILE), lambda b, i, j: (b, i, j)),
        out_shape=jax.ShapeDtypeStruct((bsz, s, p), F32),
        scratch_shapes=[pltpu.VMEM((TOK_TILE, d), BF16)],
        compiler_params=_cparams(("parallel", "parallel", "arbitrary")),
        name="in_proj",
    )(xc, g_pre, ss_tab, w_pad)


def _prep_kernel(p_ref, pp_ref, pn_ref, mu_ref, wup_ref, aup_ref, w0_ref, a0_ref, kkw_ref, kaw_ref, gup_ref,
                 ones_ref, lwf_ref, lwb_ref, kdf_ref, kdb_ref, af_ref, ab_ref, kk_ref, v_ref, r_ref, g_ref,
                 *, n_tiles, d_r):
    i = pl.program_id(1)
    tm = p_ref.shape[1]
    rows = lax.broadcasted_iota(jnp.int32, (tm, 1), 0)
    no_prev = i <= 1
    no_next = jnp.logical_or(i == 0, i == n_tiles - 1)

    def shifted(c0, c1):
        p = p_ref[0, :, c0:c1]
        hp = jnp.where(no_prev, 0.0, pp_ref[0, 7:8, c0:c1])
        hn = jnp.where(no_next, 0.0, pn_ref[0, 0:1, c0:c1])
        prev = jnp.where(rows == 0, hp, pltpu.roll(p, 1, 0))
        nxt = jnp.where(rows == tm - 1, hn, pltpu.roll(p, tm - 1, 0))
        return p + mu_ref[0:1, c0:c1] * (prev - p) + mu_ref[1:2, c0:c1] * (nxt - p)

    o_w = 2 * d_r
    o_a = o_w + 2 * W_LORA
    o_r = o_a + 2 * A_LORA
    o_g = o_r + d_r

    w_log = w0_ref[...] + _dot(jnp.tanh(shifted(o_w, o_a)).astype(BF16), wup_ref[...])
    z = -w_log
    w_log = -(jnp.maximum(z, 0.0) + jnp.log(1.0 + jnp.exp(-jnp.abs(z)))) - 0.5
    lw = -jnp.exp(w_log)
    lwf_ref[0] = lw[:, :d_r]
    lwb_ref[0] = lw[:, d_r:]
    a = _sigmoid(a0_ref[...] + _dot(shifted(o_a, o_r).astype(BF16), aup_ref[...]))
    af_ref[0] = a[:, :d_r]
    ab_ref[0] = a[:, d_r:]

    k = shifted(0, d_r)
    ka = kaw_ref[...]
    kdf_ref[0] = k * (1.0 + (a[:, :d_r] - 1.0) * ka)
    kdb_ref[0] = k * (1.0 + (a[:, d_r:] - 1.0) * ka)
    kk = k * kkw_ref[...]
    sq = kk * kk
    ss = jnp.concatenate([_split_dot(sq[:, c:c + GROUP], ones_ref[...]) for c in range(0, d_r, GROUP)], axis=1)
    kk_ref[0] = kk * lax.rsqrt(jnp.maximum(ss, KK_EPS))
    v_ref[0] = shifted(d_r, 2 * d_r)
    r_ref[0] = shifted(o_r, o_g)
    g_ref[0] = _dot(_sigmoid(shifted(o_g, o_g + gup_ref.shape[0])).astype(BF16), gup_ref[...])


def _rwkv_prep(p, pr, mu_pad, wup_bd, aup_bd, w0c, a0c, k_k, k_a, gup_pad, ones_bd, d_r):
    bsz, s, _ = p.shape
    n_tiles = s // TOK_TILE
    hb = TOK_TILE // 8
    const = lambda arr: pl.BlockSpec(arr.shape, lambda b, i: (0,) * arr.ndim)
    out = jax.ShapeDtypeStruct((bsz, s, d_r), F32)
    ospec = pl.BlockSpec((1, TOK_TILE, d_r), lambda b, i: (b, i, 0))
    return pl.pallas_call(
        functools.partial(_prep_kernel, n_tiles=n_tiles, d_r=d_r),
        grid=(bsz, n_tiles),
        in_specs=[pl.BlockSpec((1, TOK_TILE, pr), lambda b, i: (b, i, 0)),
                  pl.BlockSpec((1, 8, pr), lambda b, i: (b, jnp.maximum(i * hb - 1, 0), 0)),
                  pl.BlockSpec((1, 8, pr), lambda b, i: (b, jnp.minimum((i + 1) * hb, s // 8 - 1), 0)),
                  const(mu_pad), const(wup_bd), const(aup_bd), const(w0c), const(a0c), const(k_k), const(k_a),
                  const(gup_pad), const(ones_bd)],
        out_specs=[ospec] * 10,
        out_shape=[out] * 10,
        compiler_params=_cparams(("parallel", "parallel")),
        name="rwkv_prep",
    )(p, p, p, mu_pad, wup_bd, aup_bd, w0c, a0c, k_k, k_a, gup_pad, ones_bd)


def _scan_direction(lw, kd, a, kk, v, r, h_ref, reverse, bd_mask, eye):
    c = CHUNK
    row = lax.broadcasted_iota(jnp.int32, (c, GROUP), 0)
    pos = lax.broadcasted_iota(jnp.int32, (c, GROUP), 1) % c
    tr = lax.broadcasted_iota(jnp.int32, (c, c), 0)
    tc = lax.broadcasted_iota(jnp.int32, (c, c), 1)
    if reverse:
        strict, incl, tri = pos > row, pos >= row, (tc >= tr)
    else:
        strict, incl, tri = pos < row, pos <= row, (tc <= tr)
    cum = jnp.dot(tri.astype(F32), lw, precision=lax.Precision.HIGHEST, preferred_element_type=F32)
    tot = cum[0:1] if reverse else cum[c - 1:c]
    g_in = jnp.exp(cum)
    g_ex = jnp.exp(cum - lw)
    g_inv = jnp.exp(-cum)
    g_rem = jnp.exp(tot - cum)
    b = kk * a

    def bd(x):
        xb = x.astype(BF16)
        return jnp.where(bd_mask, jnp.concatenate([xb] * (GROUP // c), axis=0), jnp.zeros((), BF16))

    at = -kk * g_ex
    rt = r * g_in
    lr = jnp.concatenate([at, rt], axis=0).astype(BF16)
    ab_r = _dot_nt(lr, bd(b * g_inv))
    ak_r = _dot_nt(lr, bd(kd * g_inv))
    a_ab = jnp.where(strict, ab_r[:c], 0.0)
    a_rb = jnp.where(incl, ab_r[c:], 0.0).astype(BF16)
    a_ak = jnp.where(strict, ak_r[:c], 0.0).astype(BF16)
    a_rk = jnp.where(incl, ak_r[c:], 0.0).astype(BF16)

    x = jnp.where(pos == row, 1.0, 0.0) + jnp.where(pos // 2 == row // 2, a_ab, 0.0)
    m_blk = 2
    while m_blk < c:
        off = jnp.logical_and(pos // (2 * m_blk) == row // (2 * m_blk), pos // m_blk != row // m_blk)
        bd_x = bd(x)
        x = x + _dot(_dot(x.astype(BF16), bd(jnp.where(off, a_ab, 0.0))).astype(BF16), bd_x)
        m_blk *= 2
    t_inv = x.astype(BF16)

    bd_v = bd(v)
    av = _dot(a_ak, bd_v)
    wt = _dot(t_inv, bd(at))
    ut = _dot(t_inv, bd(av))
    q = rt + _dot(a_rb, bd(wt))
    y_loc = _dot(a_rb, bd(ut)) + _dot(a_rk, bd_v)
    bh = (b * g_rem).astype(BF16)
    kh = (kd * g_rem).astype(BF16)
    m = jnp.where(bd_mask, _dot_tn(bh, wt.astype(BF16)), 0.0) + jnp.where(eye, jnp.exp(tot), 0.0)
    g = jnp.where(bd_mask,
                  _dot_tn(jnp.concatenate([bh, kh], axis=0),
                          jnp.concatenate([ut, v], axis=0).astype(BF16)), 0.0)
    h0 = h_ref[...].astype(BF16)
    y = _dot(q.astype(BF16), h0) + y_loc
    h_ref[...] = _dot(m.astype(BF16), h0) + g
    return y


def _scan_kernel(lwf, kdf, af, kkf, vf, rf, lwb, kdb, ab, kkb, vb, rb, yf_ref, yb_ref, hf_ref, hb_ref):
    @pl.when(pl.program_id(2) == 0)
    def _():
        hf_ref[...] = jnp.zeros_like(hf_ref)
        hb_ref[...] = jnp.zeros_like(hb_ref)

    gr = lax.broadcasted_iota(jnp.int32, (GROUP, GROUP), 0)
    gc = lax.broadcasted_iota(jnp.int32, (GROUP, GROUP), 1)
    bd_mask = (gr // CHUNK) == (gc // CHUNK)
    eye = gr == gc
    yf_ref[0] = _scan_direction(lwf[0], kdf[0], af[0], kkf[0], vf[0], rf[0], hf_ref, False, bd_mask, eye)
    yb_ref[0] = _scan_direction(lwb[0], kdb[0], ab[0], kkb[0], vb[0], rb[0], hb_ref, True, bd_mask, eye)


def _wkv_scan(lwf, lwb, kdf, kdb, af, ab, kk, v, r, n_ctx_chunks):
    bsz, s, d_r = kk.shape
    nc = s // CHUNK

    def fwd(b, g, c):
        return (b, c, g)

    def bwd(b, g, c):
        return (b, jnp.where(c < n_ctx_chunks, n_ctx_chunks - 1 - c, nc - 1 + n_ctx_chunks - c), g)

    blk = (1, CHUNK, GROUP)
    out = jax.ShapeDtypeStruct((bsz, s, d_r), F32)
    return pl.pallas_call(
        _scan_kernel,
        grid=(bsz, d_r // GROUP, nc),
        in_specs=[pl.BlockSpec(blk, fwd)] * 6 + [pl.BlockSpec(blk, bwd)] * 6,
        out_specs=[pl.BlockSpec(blk, fwd), pl.BlockSpec(blk, bwd)],
        out_shape=[out, out],
        scratch_shapes=[pltpu.VMEM((GROUP, GROUP), F32), pltpu.VMEM((GROUP, GROUP), F32)],
        compiler_params=_cparams(("parallel", "parallel", "arbitrary")),
        name="wkv_scan",
    )(lwf, kdf, af, kk, v, r, lwb, kdb, ab, kk, v, r)


def _readout_kernel(yf_ref, yb_ref, r_ref, kdf_ref, kdb_ref, v_ref, g_ref, pb0_ref, pb1_ref, pc0_ref, pc1_ref,
                    pu0_ref, pu1_ref, x_ref, cw_ref, rk_ref, lw_ref, lb_ref, ones_ref, wout_ref, gn_ref, mod_ref,
                    wr_ref, br_ref, x1_ref, h2_ref, ti_ref, tw_ref, *, d_r):
    tm = x_ref.shape[1]
    ones = ones_ref[...]
    pair = lambda lo_ref, hi_ref: jnp.concatenate([lo_ref[0], hi_ref[0]], axis=1)

    def seg_sum(z):
        return jnp.concatenate([_split_dot(z[:, c:c + GROUP], ones) for c in range(0, d_r, GROUP)], axis=1)

    r = r_ref[0]
    v = v_ref[0]
    o = None
    for y_ref, kd_ref in ((yf_ref, kdf_ref), (yb_ref, kdb_ref)):
        y = y_ref[0]
        mu = seg_sum(y) * (1.0 / HEAD)
        yc = y - mu
        var = seg_sum(yc * yc) * (1.0 / HEAD)
        o_d = yc * lax.rsqrt(var + LNX_EPS) * lw_ref[...] + lb_ref[...]
        o_d = o_d + seg_sum(r * kd_ref[0] * rk_ref[...]) * v
        o = o_d if o is None else o + o_d
    y_rwkv = o * g_ref[0]

    rows = lax.broadcasted_iota(jnp.int32, (tm, 1), 0) % GRID_W
    cu = pair(pc0_ref, pc1_ref) * pair(pu0_ref, pu1_ref)
    prev = jnp.where(rows == 0, 0.0, pltpu.roll(cu, 1, 0))
    nxt = jnp.where(rows == GRID_W - 1, 0.0, pltpu.roll(cu, tm - 1, 0))
    y_conv = pair(pb0_ref, pb1_ref) * (prev * cw_ref[0:1] + cu * cw_ref[1:2] + nxt * cw_ref[2:3])

    mix = _dot(jnp.concatenate([y_conv, y_rwkv], axis=1).astype(BF16), wout_ref[...])
    md = mod_ref[0]
    x1 = x_ref[0] + md[0:1] * _rms(mix, gn_ref[0:1])
    x1_ref[0] = x1
    h2 = _rms(x1, gn_ref[1:2]) * (1.0 + md[2:3]) + md[1:2]
    h2_ref[0] = h2

    logits = _dot(h2.astype(BF16), wr_ref[...]) + br_ref[...]
    lane = lax.broadcasted_iota(jnp.int32, logits.shape, 1)
    work = jnp.where(lane < N_EXPERTS, logits, -jnp.inf)
    vals, idxs = [], []
    for _ in range(TOP_K):
        mx = jnp.max(work, axis=-1, keepdims=True)
        ix = jnp.min(jnp.where(work == mx, lane, LANE), axis=-1, keepdims=True)
        vals.append(mx)
        idxs.append(ix)
        work = jnp.where(lane == ix, -jnp.inf, work)
    ex = [jnp.exp(vv - vals[0]) for vv in vals]
    den = ex[0] + ex[1] + ex[2] + ex[3]
    ti = jnp.zeros(logits.shape, jnp.int32)
    tw = jnp.zeros(logits.shape, F32)
    for kx in range(TOP_K):
        ti = jnp.where(lane == kx, idxs[kx], ti)
        tw = jnp.where(lane == kx, ex[kx] / den, tw)
    ti_ref[0] = ti
    tw_ref[0] = tw


def _readout(yf, yb, r, kdf, kdb, v, g, p, x, conv_w, r_k, lnx_w, lnx_b, ones_bd, w_out, gn, mod3, wr_pad,
             br_pad, conv_col, d_r):
    bsz, t, d = x.shape
    nt = t // TOK_TILE
    cb = conv_col // N_TILE
    seq = lambda b, i: (b, i + 1, 0)
    const = lambda arr: pl.BlockSpec(arr.shape, lambda b, i: (0,) * arr.ndim)
    rspec = pl.BlockSpec((1, TOK_TILE, d_r), seq)
    pspec = lambda off: pl.BlockSpec((1, TOK_TILE, N_TILE), lambda b, i: (b, i + 1, cb + off))
    xspec = pl.BlockSpec((1, TOK_TILE, d), lambda b, i: (b, i, 0))
    lspec = pl.BlockSpec((1, TOK_TILE, LANE), lambda b, i: (b, i, 0))
    return pl.pallas_call(
        functools.partial(_readout_kernel, d_r=d_r),
        grid=(bsz, nt),
        in_specs=[rspec] * 7 + [pspec(n) for n in range(6)] + [xspec,
                  const(conv_w), const(r_k), const(lnx_w), const(lnx_b), const(ones_bd), const(w_out), const(gn),
                  pl.BlockSpec((1, 3, d), lambda b, i: (b, 0, 0)), const(wr_pad), const(br_pad)],
        out_specs=[xspec, xspec, lspec, lspec],
        out_shape=[jax.ShapeDtypeStruct((bsz, t, d), F32), jax.ShapeDtypeStruct((bsz, t, d), F32),
                   jax.ShapeDtypeStruct((bsz, t, LANE), jnp.int32), jax.ShapeDtypeStruct((bsz, t, LANE), F32)],
        compiler_params=_cparams(("parallel", "parallel")),
        name="readout",
    )(yf, yb, r, kdf, kdb, v, g, p, p, p, p, p, p, x, conv_w, r_k, lnx_w, lnx_b, ones_bd, w_out, gn, mod3, wr_pad,
      br_pad)


def _row_copy(src_hbm, dst, sem, src_row, dst_row):
    return pltpu.make_async_copy(src_hbm.at[pl.ds(src_row, 1), :], dst.at[pl.ds(dst_row, 1), :], sem)


def _gather_kernel(tok_ref, nused_ref, h_hbm, o_ref, buf, sem):
    i = pl.program_id(0)
    base = i * GATHER_ROWS

    @pl.when(base < nused_ref[0])
    def _():
        def issue(rw, carry):
            _row_copy(h_hbm, buf, sem, tok_ref[base + rw], rw).start()
            return carry

        lax.fori_loop(0, GATHER_ROWS, issue, 0)

        def drain(rw, carry):
            _row_copy(h_hbm, buf, sem, 0, rw).wait()
            return carry

        lax.fori_loop(0, GATHER_ROWS, drain, 0)
        o_ref[...] = buf[...].astype(BF16)

    @pl.when(base >= nused_ref[0])
    def _():
        o_ref[...] = jnp.zeros_like(o_ref)


def _moe_gather(slot_tok, n_used_rows, h2):
    n_slots = slot_tok.shape[0]
    d = h2.shape[1]
    return pl.pallas_call(
        _gather_kernel,
        grid_spec=pltpu.PrefetchScalarGridSpec(
            num_scalar_prefetch=2,
            grid=(n_slots // GATHER_ROWS,),
            in_specs=[pl.BlockSpec(memory_space=pl.ANY)],
            out_specs=pl.BlockSpec((GATHER_ROWS, d), lambda i, tok, nu: (i, 0)),
            scratch_shapes=[pltpu.VMEM((GATHER_ROWS, d), F32), pltpu.SemaphoreType.DMA(())]),
        out_shape=jax.ShapeDtypeStruct((n_slots, d), BF16),
        compiler_params=_cparams(("arbitrary",)),
        name="moe_gather",
    )(slot_tok, n_used_rows, h2)


def _gu_kernel(be_ref, nub_ref, x_ref, wg_ref, wl_ref, bg_ref, bl_ref, o_ref):
    @pl.when(pl.program_id(0) < nub_ref[0])
    def _():
        x = x_ref[...]
        glu = jnp.minimum(_dot(x, wg_ref[0]) + bg_ref[0], SWIGLU_LIMIT)
        lin = jnp.clip(_dot(x, wl_ref[0]) + bl_ref[0], -SWIGLU_LIMIT, SWIGLU_LIMIT)
        o_ref[...] = (glu * _sigmoid(SWIGLU_ALPHA * glu) * (lin + 1.0)).astype(BF16)

    @pl.when(pl.program_id(0) >= nub_ref[0])
    def _():
        o_ref[...] = jnp.zeros_like(o_ref)


def _dn_kernel(be_ref, nub_ref, a_ref, w_ref, b_ref, sw_ref, o_ref):
    @pl.when(pl.program_id(0) < nub_ref[0])
    def _():
        o_ref[...] = (_dot(a_ref[...], w_ref[0]) + b_ref[0]) * sw_ref[...]

    @pl.when(pl.program_id(0) >= nub_ref[0])
    def _():
        o_ref[...] = jnp.zeros_like(o_ref)


def _expert_maps(nt):
    def wmap(i, j, be, nub):
        live = i < nub[0]
        return (be[jnp.minimum(i, nub[0] - 1)], 0, jnp.where(live, j, nt - 1))

    return wmap


def _moe_gu(block_e, n_used_blocks, xs, w_glu, w_lin, b_glu, b_lin):
    n_slots, d = xs.shape
    de = w_glu.shape[2]
    nt = de // MOE_TN
    wmap = _expert_maps(nt)
    return pl.pallas_call(
        _gu_kernel,
        grid_spec=pltpu.PrefetchScalarGridSpec(
            num_scalar_prefetch=2,
            grid=(n_slots // MOE_TM, nt),
            in_specs=[pl.BlockSpec((MOE_TM, d), lambda i, j, be, nub: (i, 0)),
                      pl.BlockSpec((1, d, MOE_TN), wmap), pl.BlockSpec((1, d, MOE_TN), wmap),
                      pl.BlockSpec((1, 1, MOE_TN), wmap), pl.BlockSpec((1, 1, MOE_TN), wmap)],
            out_specs=pl.BlockSpec((MOE_TM, MOE_TN), lambda i, j, be, nub: (i, j))),
        out_shape=jax.ShapeDtypeStruct((n_slots, de), BF16),
        compiler_params=_cparams(("arbitrary", "arbitrary")),
        name="moe_gu",
    )(block_e, n_used_blocks, xs, w_glu, w_lin, b_glu, b_lin)


def _moe_dn(block_e, n_used_blocks, act, w_dn, b_dn, slot_w):
    n_slots, de = act.shape
    d = w_dn.shape[2]
    nt = d // MOE_TN
    wmap = _expert_maps(nt)
    return pl.pallas_call(
        _dn_kernel,
        grid_spec=pltpu.PrefetchScalarGridSpec(
            num_scalar_prefetch=2,
            grid=(n_slots // MOE_TM, nt),
            in_specs=[pl.BlockSpec((MOE_TM, de), lambda i, j, be, nub: (i, 0)),
                      pl.BlockSpec((1, de, MOE_TN), wmap), pl.BlockSpec((1, 1, MOE_TN), wmap),
                      pl.BlockSpec((MOE_TM, 1), lambda i, j, be, nub: (i, 0))],
            out_specs=pl.BlockSpec((MOE_TM, MOE_TN), lambda i, j, be, nub: (i, j))),
        out_shape=jax.ShapeDtypeStruct((n_slots, d), F32),
        compiler_params=_cparams(("arbitrary", "arbitrary")),
        name="moe_dn",
    )(block_e, n_used_blocks, act, w_dn, b_dn, slot_w)


def _combine_kernel(pos_ref, y_hbm, x1_ref, gate_ref, gn_ref, o_ref, buf, sem):
    i = pl.program_id(1) + pl.program_id(0) * pl.num_programs(1)
    base = i * (COMBINE_TOK * TOP_K)

    def issue(n, carry):
        for kx in range(TOP_K):
            _row_copy(y_hbm, buf.at[kx], sem, pos_ref[base + n * TOP_K + kx], n).start()
        return carry

    lax.fori_loop(0, COMBINE_TOK, issue, 0)

    def drain(n, carry):
        for kx in range(TOP_K):
            _row_copy(y_hbm, buf.at[kx], sem, 0, n).wait()
        return carry

    lax.fori_loop(0, COMBINE_TOK, drain, 0)
    ffn = (buf[0] + buf[1]) + (buf[2] + buf[3])
    o_ref[0] = x1_ref[0] + gate_ref[0] * _rms(ffn, gn_ref[...])


def _moe_combine(pos, y_sorted, x1, gate2, g_post):
    bsz, t, d = x1.shape
    xspec = pl.BlockSpec((1, COMBINE_TOK, d), lambda b, i, pos: (b, i, 0))
    return pl.pallas_call(
        _combine_kernel,
        grid_spec=pltpu.PrefetchScalarGridSpec(
            num_scalar_prefetch=1,
            grid=(bsz, t // COMBINE_TOK),
            in_specs=[pl.BlockSpec(memory_space=pl.ANY), xspec,
                      pl.BlockSpec((1, 1, d), lambda b, i, pos: (b, 0, 0)),
                      pl.BlockSpec((1, d), lambda b, i, pos: (0, 0))],
            out_specs=xspec,
            scratch_shapes=[pltpu.VMEM((TOP_K, COMBINE_TOK, d), F32), pltpu.SemaphoreType.DMA(())]),
        out_shape=jax.ShapeDtypeStruct((bsz, t, d), F32),
        compiler_params=_cparams(("arbitrary", "arbitrary")),
        name="moe_combine",
    )(pos, y_sorted, x1, gate2, g_post)


def _route(top_idx, top_w):
    n_assign = top_idx.size
    e_flat = top_idx.reshape(-1)
    order = jnp.argsort(e_flat, stable=True).astype(jnp.int32)
    e_sorted = e_flat[order]
    counts = jnp.bincount(e_flat, length=N_EXPERTS).astype(jnp.int32)
    padded = ((counts + MOE_TM - 1) // MOE_TM) * MOE_TM
    start = jnp.cumsum(counts) - counts
    pend = jnp.cumsum(padded)
    pstart = pend - padded
    dest = pstart[e_sorted] + (jnp.arange(n_assign, dtype=jnp.int32) - start[e_sorted])
    n_blocks = n_assign // MOE_TM + N_EXPERTS
    n_slots = n_blocks * MOE_TM
    slot_tok = jnp.zeros((n_slots,), jnp.int32).at[dest].set(order // TOP_K)
    slot_w = jnp.zeros((n_slots,), F32).at[dest].set(top_w.reshape(-1)[order])
    pos = jnp.zeros((n_assign,), jnp.int32).at[order].set(dest)
    block_first = jnp.arange(n_blocks, dtype=jnp.int32) * MOE_TM
    block_e = jnp.minimum(jnp.sum(block_first[:, None] >= pend[None, :], axis=-1), N_EXPERTS - 1).astype(jnp.int32)
    n_used_rows = pend[-1:].astype(jnp.int32)
    return slot_tok, slot_w.reshape(n_slots, 1), pos, block_e, n_used_rows, n_used_rows // MOE_TM


def _block_diag2(w):
    z = jnp.zeros_like(w[0])
    return jnp.concatenate([jnp.concatenate([w[0], z], axis=1), jnp.concatenate([z, w[1]], axis=1)], axis=0)


def kernel(x, c, ctx, c_ctx, w_mod, b_mod, norm_g, w_in, conv_w, mu_shift, w0, w_up, a0, a_up, k_k, k_a, r_k,
           lnx_w, lnx_b, g_up, w_out, w_router, b_router, w_gu, b_gu, w_dn, b_dn):
    bsz, t, d = x.shape
    n_ctx = ctx.shape[1]
    d_r = k_k.shape[-1]
    d_conv = conv_w.shape[-1]
    p_rwkv = mu_shift.shape[-1]
    assert w_mod.shape[0] == 1, "single layer"
    assert n_ctx == TOK_TILE and t % TOK_TILE == 0 and d_conv == d_r
    l = 0
    g_pre_mix, g_post_mix, g_pre_ffn, g_post_ffn = (norm_g[l, n].reshape(1, d) for n in range(4))

    c_rows = jnp.concatenate([c, c_ctx[None], jnp.zeros((8 - bsz - 1, d), F32)], axis=0)
    mod = _modulation(c_rows, w_mod[l], b_mod[l])
    lat = mod[:bsz].reshape(bsz, 6, d)
    cmod = jnp.broadcast_to(mod[bsz].reshape(1, 6, d), (bsz, 6, d))
    ss_tab = jnp.stack([cmod[:, 0:2], lat[:, 0:2]], axis=1)
    mod3 = jnp.stack([lat[:, 2], lat[:, 3], lat[:, 4]], axis=1)
    gate2 = lat[:, 5].reshape(bsz, 1, d)

    pr = -(-p_rwkv // N_TILE) * N_TILE
    conv_col = pr
    assert d_conv == 2 * N_TILE
    wi = w_in[l].astype(BF16)
    w_pad = jnp.concatenate([wi[:, 3 * d_conv:], jnp.zeros((d, conv_col - p_rwkv), BF16), wi[:, :3 * d_conv]], axis=1)
    xc = jnp.concatenate([ctx, x], axis=1)
    p = _in_proj(xc, g_pre_mix, ss_tab, w_pad)

    mu_pad = jnp.pad(mu_shift[l], ((0, 0), (0, pr - p_rwkv)))
    gw = -(-G_LORA // LANE) * LANE
    gup_pad = jnp.pad(g_up[l], ((0, gw - G_LORA), (0, 0))).astype(BF16)
    hi = lax.broadcasted_iota(jnp.int32, (GROUP, GROUP), 0) // HEAD
    hj = lax.broadcasted_iota(jnp.int32, (GROUP, GROUP), 1) // HEAD
    ones_bd = (hi == hj).astype(BF16)
    lwf, lwb, kdf, kdb, af, ab, kk, v, r, g = _rwkv_prep(
        p, pr, mu_pad, _block_diag2(w_up[l]).astype(BF16), _block_diag2(a_up[l]).astype(BF16),
        w0[l].reshape(1, 2 * d_r), a0[l].reshape(1, 2 * d_r), k_k[l].reshape(1, d_r), k_a[l].reshape(1, d_r),
        gup_pad, ones_bd, d_r)

    yf, yb = _wkv_scan(lwf, lwb, kdf, kdb, af, ab, kk, v, r, n_ctx // CHUNK)

    wr_pad = jnp.pad(w_router[l], ((0, 0), (0, LANE - N_EXPERTS))).astype(BF16)
    br_pad = jnp.pad(b_router[l], (0, LANE - N_EXPERTS)).reshape(1, LANE)
    gn = jnp.concatenate([g_post_mix, g_pre_ffn], axis=0)
    x1, h2, top_i, top_w = _readout(
        yf, yb, r, kdf, kdb, v, g, p, x, conv_w[l], r_k[l].reshape(1, d_r), lnx_w[l].reshape(1, d_r),
        lnx_b[l].reshape(1, d_r), ones_bd, w_out[l].astype(BF16), gn, mod3, wr_pad, br_pad, conv_col, d_r)

    n_tok = bsz * t
    slot_tok, slot_w, pos, block_e, n_used_rows, n_used_blocks = _route(
        top_i.reshape(n_tok, LANE)[:, :TOP_K], top_w.reshape(n_tok, LANE)[:, :TOP_K])
    xs = _moe_gather(slot_tok, n_used_rows, h2.reshape(n_tok, d))
    de = w_dn.shape[2]
    wgu = w_gu[l].reshape(N_EXPERTS, d, de, 2)
    bgu = b_gu[l].reshape(N_EXPERTS, 1, de, 2)
    act = _moe_gu(block_e, n_used_blocks, xs, wgu[..., 0].astype(BF16), wgu[..., 1].astype(BF16),
                  bgu[..., 0], bgu[..., 1])
    y_sorted = _moe_dn(block_e, n_used_blocks, act, w_dn[l].astype(BF16), b_dn[l].reshape(N_EXPERTS, 1, d), slot_w)
    return _moe_combine(pos, y_sorted, x1, gate2, g_post_ffn)
```
